```python
import math
import jax, jax.numpy as jnp
from jax import lax
import numpy as np

D_MODEL = 2048
BATCH = 32
SEQ = 256
DEPTH = 2
DEC_BATCH = 8
DEC_SEQ = 2048
PAST_LEN = 256

GRID_W = 64
CHUNK = 128
Q_BLOCK = 128
EPS = 1e-6
N_MOD = 6
A_GROUPS = 4
A_HEAD = 128
A_WIDTH = A_GROUPS * A_HEAD
MLA_HEADS = 8
Q_LORA = 512
KV_LORA = 256
NOPE_DIM = 128
ROPE_DIM = 64
QK_DIM = NOPE_DIM + ROPE_DIM
V_DIM = 128
MLA_WIDTH = MLA_HEADS * V_DIM
ROPE_THETA = 10000.0
RET_HEADS = 4
RET_DK = 128
RET_DV = 128
RET_WIDTH = RET_HEADS * RET_DV
IN_SIZES = (A_WIDTH, A_WIDTH, Q_LORA, KV_LORA, ROPE_DIM,
            RET_HEADS * RET_DK, RET_HEADS * RET_DK, RET_WIDTH, RET_WIDTH)
IN_COLS = sum(IN_SIZES)
MIX_WIDTH = A_WIDTH + MLA_WIDTH + RET_WIDTH
N_GROUPS = 4
EXPERTS_PER_GROUP = 8
N_EXPERTS = N_GROUPS * EXPERTS_PER_GROUP
TOP_K = 2
D_EXPERT = 512

kernel_name = 'hybrid_gmlp_mla_retention_hmoe_diffusion_step'


def rms_norm(x, g):
    xf = x.astype(jnp.float32)
    y = xf * lax.rsqrt(jnp.mean(xf * xf, axis=-1, keepdims=True) + EPS)
    return (y * g.astype(jnp.float32)).astype(x.dtype)


def chunk_mlp(u, v, v_g, w_s, b_s):
    b, l, _ = u.shape
    nc = l // CHUNK
    u = jax.nn.gelu(u).reshape(b, nc, CHUNK, A_GROUPS, A_HEAD)
    v = rms_norm(jax.nn.gelu(v).reshape(b, nc, CHUNK, A_GROUPS, A_HEAD), v_g.reshape(A_GROUPS, A_HEAD))
    s = jnp.einsum('gij,bcjgd->bcigd', w_s, v) + b_s.T[:, :, None]
    return (u * s).reshape(b, l, A_WIDTH)


def axial_rope(x, n_tokens):
    rows = n_tokens // GRID_W
    t = jnp.arange(rows * GRID_W)
    row, col = t // GRID_W, t % GRID_W
    half = ROPE_DIM // 2
    quarter = half // 2
    inv_freq = 1.0 / (ROPE_THETA ** (jnp.arange(quarter, dtype=jnp.float32) / quarter))
    xf = x.astype(jnp.float32)

    def rotate(xa, pos):
        ang = pos.astype(jnp.float32)[:, None] * inv_freq[None, :]
        cos, sin = jnp.cos(ang)[:, None, :], jnp.sin(ang)[:, None, :]
        x1, x2 = xa[..., :quarter], xa[..., quarter:]
        return jnp.concatenate([x1 * cos - x2 * sin, x1 * sin + x2 * cos], axis=-1)

    return jnp.concatenate([rotate(xf[..., :half], row), rotate(xf[..., half:], col)], axis=-1).astype(x.dtype)


def rope_latent(t, n_tokens):
    return jnp.concatenate([t[..., :NOPE_DIM], axial_rope(t[..., NOPE_DIM:], n_tokens)], axis=-1)


def mla_keys_values(ckv, k_rope, w_kvb, kn_g):
    b, l, _ = ckv.shape
    kv = (ckv @ w_kvb).reshape(b, l, MLA_HEADS, NOPE_DIM + V_DIM)
    k_r = jnp.broadcast_to(k_rope[:, :, None, :], (b, l, MLA_HEADS, ROPE_DIM)).astype(kv.dtype)
    k = rms_norm(jnp.concatenate([kv[..., :NOPE_DIM], k_r], axis=-1), kn_g)
    return k, kv[..., NOPE_DIM:]


def blocked_attention(q, k, v):
    b, lq, h, d = q.shape
    qb = q.reshape(b, lq // Q_BLOCK, Q_BLOCK, h, d).transpose(1, 0, 2, 3, 4)
    scale = d ** -0.5
    kf, vf = k.astype(jnp.float32), v.astype(jnp.float32)

    def one_block(q_blk):
        s = jnp.einsum('bqhd,bkhd->bhqk', q_blk.astype(jnp.float32), kf) * scale
        p = jax.nn.softmax(s, axis=-1)
        return jnp.einsum('bhqk,bkhd->bqhd', p, vf).astype(v.dtype)

    o = lax.map(one_block, qb)
    return o.transpose(1, 0, 2, 3, 4).reshape(b, lq, h, -1)


def retention_chunkwise(q, k, v, log_g, s0):
    b, l, h, _ = q.shape
    nc = l // CHUNK

    def chunks(t):
        return t.astype(jnp.float32).reshape(b, nc, CHUNK, h, t.shape[-1]).transpose(1, 0, 3, 2, 4)

    pos = jnp.arange(CHUNK, dtype=jnp.float32)
    diff = pos[:, None] - pos[None, :]
    decay_mask = jnp.where(diff >= 0, jnp.exp(log_g[:, None, None] * jnp.maximum(diff, 0.0)), 0.0)
    q_dec = jnp.exp(log_g[:, None] * (pos + 1.0))[:, :, None]
    k_dec = jnp.exp(log_g[:, None] * (CHUNK - 1.0 - pos))[:, :, None]
    c_dec = jnp.exp(log_g * CHUNK)[:, None, None]

    def step(s, qkv):
        qc, kc, vc = qkv
        att = jnp.einsum('bhid,bhjd->bhij', qc, kc) * decay_mask
        o = jnp.einsum('bhij,bhjv->bhiv', att, vc) + jnp.einsum('bhid,bhdv->bhiv', qc * q_dec, s)
        s = s * c_dec + jnp.einsum('bhjd,bhjv->bhdv', kc * k_dec, vc)
        return s, o

    s_fin, o = lax.scan(step, s0.astype(jnp.float32), (chunks(q), chunks(k), chunks(v)))
    return o.transpose(1, 0, 3, 2, 4).reshape(b, l, h, -1), s_fin


def bidir_retention(q, k, v, decay_logit, s0):
    log_f = jax.nn.log_sigmoid(decay_logit[0].astype(jnp.float32))
    log_b = jax.nn.log_sigmoid(decay_logit[1].astype(jnp.float32))
    o_f, s_f = retention_chunkwise(q, k, v, log_f, s0[:, 0])
    o_b, s_b = retention_chunkwise(jnp.flip(q, 1), jnp.flip(k, 1), jnp.flip(v, 1), log_b, s0[:, 1])
    return o_f + jnp.flip(o_b, 1), jnp.stack([s_f, s_b], axis=1)


def token_mixing(h, w_in_l, gm_vg, gm_ws, gm_bs, q_g, w_qb, kv_g, w_kvb, qn_g, kn_g, r_dec, r_g, w_o, ctx):
    b, l, _ = h.shape
    splits = np.cumsum(IN_SIZES)[:-1].tolist()
    a_u, a_v, m_q, m_kv, m_kr, r_q, r_k, r_v, r_gate = jnp.split(h @ w_in_l, splits, axis=-1)
    y_a = chunk_mlp(a_u, a_v, gm_vg, gm_ws, gm_bs)
    q = rms_norm((rms_norm(m_q, q_g) @ w_qb).reshape(b, l, MLA_HEADS, QK_DIM), qn_g)
    ckv = rms_norm(m_kv, kv_g)
    k, v = mla_keys_values(ckv, m_kr, w_kvb, kn_g)
    rq = r_q.reshape(b, l, RET_HEADS, RET_DK)
    rk = r_k.reshape(b, l, RET_HEADS, RET_DK) * (RET_DK ** -0.5)
    rv = r_v.reshape(b, l, RET_HEADS, RET_DV)
    if ctx is None:
        s0 = jnp.zeros((b, 2, RET_HEADS, RET_DK, RET_DV), jnp.float32)
    else:
        ckv_c, kr_c, s0 = ctx
        q = rope_latent(q, l)
        k = rope_latent(k, l)
        k_c, v_c = mla_keys_values(ckv_c, kr_c, w_kvb, kn_g)
        k = jnp.concatenate([k, k_c.astype(k.dtype)], axis=1)
        v = jnp.concatenate([v, v_c.astype(v.dtype)], axis=1)
    y_b = blocked_attention(q, k, v).reshape(b, l, MLA_WIDTH)
    o_r, st = bidir_retention(rq, rk, rv, r_dec, s0)
    y_c = rms_norm(o_r, r_g.reshape(RET_HEADS, RET_DV)).astype(h.dtype).reshape(b, l, RET_WIDTH) * jax.nn.silu(r_gate)
    out = jnp.concatenate([y_a, y_b, y_c], axis=-1) @ w_o
    return out, (ckv, m_kr, st)


def hier_moe(h, wg, bg, we, be, w1, w3, w2):
    b, l, d = h.shape
    t = h.reshape(b * l, d)
    g_prob = jax.nn.softmax((t @ wg + bg).astype(jnp.float32), axis=-1)
    g_p, g_idx = lax.top_k(g_prob, 1)
    e_logits = (t @ we + be).astype(jnp.float32).reshape(-1, N_GROUPS, EXPERTS_PER_GROUP)
    e_in = e_logits[jnp.arange(t.shape[0]), g_idx[:, 0]]
    e_p, e_idx = lax.top_k(jax.nn.softmax(e_in, axis=-1), TOP_K)
    w = g_p * e_p / jnp.sum(e_p, axis=-1, keepdims=True)
    ids = g_idx * EXPERTS_PER_GROUP + e_idx
    combine = jnp.einsum('tk,tke->te', w, jax.nn.one_hot(ids, N_EXPERTS, dtype=jnp.float32))
    y = jnp.zeros((t.shape[0], d), jnp.float32)
    for e in range(N_EXPERTS):
        hid = jax.nn.silu(t @ w1[e]) * (t @ w3[e])
        y = y + combine[:, e:e + 1] * (hid @ w2[e])
    return y.astype(h.dtype).reshape(b, l, d)


def trunk_layer(x, cond, lw, ctx):
    (n1, n2, mw, mb, w_in_l, gm_vg, gm_ws, gm_bs, q_g, w_qb, kv_g, w_kvb, qn_g, kn_g,
     r_dec, r_g, w_o, wg, bg, we, be, w1, w3, w2) = lw
    mods = jnp.split(jax.nn.silu(cond) @ mw + mb, N_MOD, axis=-1)
    sh1, sc1, g1, sh2, sc2, g2 = [m[:, None, :] for m in mods]
    h = rms_norm(x, n1) * (1 + sc1) + sh1
    mix, ctx_tensors = token_mixing(h, w_in_l, gm_vg, gm_ws, gm_bs, q_g, w_qb, kv_g, w_kvb,
                                    qn_g, kn_g, r_dec, r_g, w_o, ctx)
    x = x + g1 * mix
    h = rms_norm(x, n2) * (1 + sc2) + sh2
    x = x + g2 * hier_moe(h, wg, bg, we, be, w1, w3, w2)
    return x, ctx_tensors


def setup_inputs(seed: int = 0) -> dict:
    key = jax.random.key(seed)
    ks = iter(jax.random.split(key, 48))

    def nrm(shape, scale):
        return jax.random.normal(next(ks), shape, jnp.float32) * scale

    def gain(shape):
        return 1.0 + nrm(shape, 0.02)

    gam = 1.0 - jnp.exp(jnp.linspace(math.log(1.0 / 32), math.log(1.0 / 512), RET_HEADS))
    logit = jnp.log(gam) - jnp.log1p(-gam)
    dm = D_MODEL ** -0.5
    return {
        'x_prompt': nrm((BATCH, SEQ, D_MODEL), 1.0),
        'x_sample': nrm((DEC_BATCH, DEC_SEQ, D_MODEL), 1.0),
        'cache_mla_ckv': nrm((DEC_BATCH, DEPTH, PAST_LEN, KV_LORA), 1.0),
        'cache_mla_krope': nrm((DEC_BATCH, DEPTH, PAST_LEN, ROPE_DIM), 1.0),
        'state_ret': nrm((DEC_BATCH, DEPTH, 2, RET_HEADS, RET_DK, RET_DV), 1.0),
        'c': nrm((DEC_BATCH, D_MODEL), 1.0),
        'c_ctx': nrm((D_MODEL,), 1.0),
        'norm1_g': gain((DEPTH, D_MODEL)),
        'norm2_g': gain((DEPTH, D_MODEL)),
        'mod_w': nrm((DEPTH, D_MODEL, N_MOD * D_MODEL), 0.5 * dm),
        'mod_b': nrm((DEPTH, N_MOD * D_MODEL), 0.02),
        'w_in': nrm((DEPTH, D_MODEL, IN_COLS), dm),
        'gm_v_g': gain((DEPTH, A_WIDTH)),
        'gm_ws': nrm((DEPTH, A_GROUPS, CHUNK, CHUNK), CHUNK ** -0.5),
        'gm_bs': gain((DEPTH, A_GROUPS, CHUNK)),
        'mla_q_g': gain((DEPTH, Q_LORA)),
        'mla_w_qb': nrm((DEPTH, Q_LORA, MLA_HEADS * QK_DIM), Q_LORA ** -0.5),
        'mla_kv_g': gain((DEPTH, KV_LORA)),
        'mla_w_kvb': nrm((DEPTH, KV_LORA, MLA_HEADS * (NOPE_DIM + V_DIM)), KV_LORA ** -0.5),
        'mla_qn_g': gain((DEPTH, QK_DIM)),
        'mla_kn_g': gain((DEPTH, QK_DIM)),
        'ret_decay': logit[None, None, :] + nrm((DEPTH, 2, RET_HEADS), 0.02),
        'ret_g': gain((DEPTH, RET_WIDTH)),
        'w_out': nrm((DEPTH, MIX_WIDTH, D_MODEL), MIX_WIDTH ** -0.5),
        'moe_wg': nrm((DEPTH, D_MODEL, N_GROUPS), dm),
        'moe_bg': nrm((DEPTH, N_GROUPS), 0.01),
        'moe_we': nrm((DEPTH, D_MODEL, N_EXPERTS), dm),
        'moe_be': nrm((DEPTH, N_EXPERTS), 0.01),
        'moe_w1': nrm((DEPTH, N_EXPERTS, D_MODEL, D_EXPERT), dm),
        'moe_w3': nrm((DEPTH, N_EXPERTS, D_MODEL, D_EXPERT), dm),
        'moe_w2': nrm((DEPTH, N_EXPERTS, D_EXPERT, D_MODEL), D_EXPERT ** -0.5),
    }


def reference(x_prompt, x_sample, cache_mla_ckv, cache_mla_krope, state_ret, c, c_ctx,
              norm1_g, norm2_g, mod_w, mod_b, w_in, gm_v_g, gm_ws, gm_bs,
              mla_q_g, mla_w_qb, mla_kv_g, mla_w_kvb, mla_qn_g, mla_kn_g,
              ret_decay, ret_g, w_out, moe_wg, moe_bg, moe_we, moe_be, moe_w1, moe_w3, moe_w2):
    xp, xs = x_prompt, x_sample
    ckv_l, kr_l, st_l = [], [], []
    for i in range(DEPTH):
        lw = (norm1_g[i], norm2_g[i], mod_w[i], mod_b[i], w_in[i], gm_v_g[i], gm_ws[i], gm_bs[i],
              mla_q_g[i], mla_w_qb[i], mla_kv_g[i], mla_w_kvb[i], mla_qn_g[i], mla_kn_g[i],
              ret_decay[i], ret_g[i], w_out[i], moe_wg[i], moe_bg[i], moe_we[i], moe_be[i],
              moe_w1[i], moe_w3[i], moe_w2[i])
        xp, (ckv, kr, st) = trunk_layer(xp, c_ctx[None, :], lw, None)
        ckv_l.append(ckv)
        kr_l.append(kr)
        st_l.append(st)
        xs, _ = trunk_layer(xs, c, lw, (cache_mla_ckv[:, i], cache_mla_krope[:, i], state_ret[:, i]))
    return (xp, xs, jnp.stack(ckv_l, axis=1), jnp.stack(kr_l, axis=1), jnp.stack(st_l, axis=1))
```

```python
import functools
import math

import numpy as np
import jax
import jax.numpy as jnp
from jax import lax
from jax.experimental import pallas as pl
from jax.experimental.pallas import tpu as pltpu

F32 = jnp.float32
BF16 = jnp.bfloat16
U32 = jnp.uint32
EPS = 1e-6

GRID_W = 64
CHUNK = 128
N_MOD = 6
A_GROUPS = 4
A_HEAD = 128
A_WIDTH = A_GROUPS * A_HEAD
MLA_HEADS = 8
Q_LORA = 512
KV_LORA = 256
NOPE_DIM = 128
ROPE_DIM = 64
QK_DIM = NOPE_DIM + ROPE_DIM
V_DIM = 128
ROPE_THETA = 10000.0
RET_HEADS = 4
RET_DK = 128
RET_DV = 128
N_GROUPS = 4
EXPERTS_PER_GROUP = 8
N_EXPERTS = N_GROUPS * EXPERTS_PER_GROUP

LANES = 128
HEAD_PAD = 2 * LANES
MIB = 1024 * 1024

COL_A = 0
COL_Q = 2 * A_WIDTH
COL_KV = COL_Q + Q_LORA
COL_R = COL_KV + 512
IN_COLS_PAD = COL_R + 4 * RET_HEADS * RET_DK
KV_BLOCK = 512

TOKEN_TILE = 512
EXPERT_TILE = 256
GATHER_CHUNK = 512


def _params(semantics, vmem_mib=48):
    return pltpu.CompilerParams(dimension_semantics=semantics, vmem_limit_bytes=vmem_mib * MIB)


def _rope_lane_sources():
    src = np.full((LANES,), -1, np.int32)
    for d in range(ROPE_DIM):
        half, part, i = d // 32, (d % 32) // 16, d % 16
        src[part * 64 + half * 16 + i] = d
    return src


def _take_cols(w, src):
    ext = jnp.concatenate([w, jnp.zeros(w.shape[:-1] + (1,), w.dtype)], axis=-1)
    idx = np.where(src < 0, w.shape[-1], src)
    return jnp.take(ext, jnp.asarray(idx), axis=-1)


def _head_pad_sources():
    rope = _rope_lane_sources()
    out = []
    for h in range(MLA_HEADS):
        out.append(h * QK_DIM + np.arange(NOPE_DIM))
        out.append(np.where(rope < 0, -1, h * QK_DIM + NOPE_DIM + rope))
    return np.concatenate(out).astype(np.int32)


def _rope_tables(n_prompt_rows, n_sample):
    t = jnp.arange(n_sample)
    row, col = (t // GRID_W).astype(F32), (t % GRID_W).astype(F32)
    quarter = ROPE_DIM // 4
    inv_freq = 1.0 / (ROPE_THETA ** (jnp.arange(quarter, dtype=F32) / quarter))
    ang = jnp.concatenate([row[:, None] * inv_freq[None, :], col[:, None] * inv_freq[None, :]], axis=1)
    cos, sin = jnp.cos(ang), jnp.sin(ang)
    one = jnp.ones((n_sample, 32), F32)
    zero = jnp.zeros((n_sample, 32), F32)
    c = jnp.concatenate([cos, one, cos, one], axis=1)
    s = jnp.concatenate([-sin, zero, sin, zero], axis=1)
    c = jnp.concatenate([jnp.ones((n_prompt_rows, LANES), F32), c], axis=0)
    s = jnp.concatenate([jnp.zeros((n_prompt_rows, LANES), F32), s], axis=0)
    return c, s


def _rms(x, n):
    return x * lax.rsqrt(jnp.sum(x * x, axis=-1, keepdims=True) * (1.0 / n) + EPS)


def _gelu_tanh(x):
    return x * (0.5 * (1.0 + jnp.tanh(math.sqrt(2.0 / math.pi) * (x + 0.044715 * (x * x * x)))))


def _silu(x):
    return x * jax.nn.sigmoid(x)


def _pack_bf16_pairs(x):
    n = x.shape[1] // 2
    lo = pltpu.bitcast(x[:, :n].astype(BF16).astype(F32), U32)
    hi = pltpu.bitcast(x[:, n:].astype(BF16).astype(F32), U32)
    return (hi & jnp.uint32(0xFFFF0000)) | (lo >> 16)


def _unpack_lo(p):
    return pltpu.bitcast(p << 16, F32)


def _unpack_hi(p):
    return pltpu.bitcast(p & jnp.uint32(0xFFFF0000), F32)


def _mod_kernel(c_ref, w_ref, b_ref, o_ref):
    a = _silu(c_ref[...]).astype(BF16)
    o_ref[...] = jnp.dot(a, w_ref[...].astype(BF16), preferred_element_type=F32) + b_ref[...]


def _modulations(cond, mod_w, mod_b):
    depth, d, n = mod_w.shape
    rows = cond.shape[0]
    tn = 1024
    return pl.pallas_call(
        _mod_kernel,
        grid=(depth, n // tn),
        in_specs=[pl.BlockSpec((rows, d), lambda l, j: (0, 0)),
                  pl.BlockSpec((None, d, tn), lambda l, j: (l, 0, j)),
                  pl.BlockSpec((None, 1, tn), lambda l, j: (l, 0, j))],
        out_specs=pl.BlockSpec((None, rows, tn), lambda l, j: (l, 0, j)),
        out_shape=jax.ShapeDtypeStruct((depth, rows, n), F32),
        compiler_params=_params(("arbitrary", "arbitrary")),
        name="modulations",
    )(cond, mod_w, mod_b.reshape(depth, 1, n))


class _Stream:
    def __init__(self, batch, seq, dec_batch, dec_seq, d_model):
        self.batch, self.seq, self.dec_batch, self.dec_seq, self.d = batch, seq, dec_batch, dec_seq, d_model
        self.n_prompt = batch * seq
        self.n_sample = dec_batch * dec_seq
        self.t = self.n_prompt + self.n_sample
        tm = TOKEN_TILE
        assert self.n_prompt % tm == 0 and dec_seq % tm == 0 and self.n_prompt % dec_seq == 0
        assert seq % CHUNK == 0 and dec_seq % CHUNK == 0
        self.tm = tm
        self.prompt_tiles = self.n_prompt // tm
        self.tiles_per_sample = dec_seq // tm
        self.n_tiles = self.t // tm

    def mod_row(self, i):
        return jnp.where(i < self.prompt_tiles, 0, 1 + (i - self.prompt_tiles) // self.tiles_per_sample)

    def rope_block(self, i):
        return jnp.where(i < self.prompt_tiles, 0, 1 + (i - self.prompt_tiles) % self.tiles_per_sample)

    def mod_spec(self, layer, which, with_j=True):
        d = self.d
        if with_j:
            return pl.BlockSpec((None, None, None, 1, d),
                                lambda i, j: (layer, which, self.mod_row(i), 0, 0))
        return pl.BlockSpec((None, None, None, 1, d), lambda i: (layer, which, self.mod_row(i), 0, 0))


def _in_proj_kernel(x_ref, n_ref, sh_ref, sc_ref, w_ref, y_ref, kv_ref, h_scr, *, kv_step):
    j = pl.program_id(1)

    @pl.when(j == 0)
    def _():
        x = x_ref[...]
        h = _rms(x, x.shape[1]) * n_ref[...] * (1.0 + sc_ref[...]) + sh_ref[...]
        h_scr[...] = h.astype(BF16)

    acc = jnp.dot(h_scr[...], w_ref[...], preferred_element_type=F32)
    y_ref[...] = acc.astype(BF16)

    @pl.when(j == kv_step)
    def _():
        kv_ref[...] = acc


def _in_proj(st, x, norm_g, mods5, layer, w_in_r):
    tm, d = st.tm, st.d
    tn = KV_BLOCK
    n = w_in_r.shape[1]
    return pl.pallas_call(
        functools.partial(_in_proj_kernel, kv_step=COL_KV // tn),
        grid=(st.n_tiles, n // tn),
        in_specs=[pl.BlockSpec((tm, d), lambda i, j: (i, 0)),
                  pl.BlockSpec((1, d), lambda i, j: (0, 0)),
                  st.mod_spec(layer, 0), st.mod_spec(layer, 1),
                  pl.BlockSpec((d, tn), lambda i, j: (0, j))],
        out_specs=[pl.BlockSpec((tm, tn), lambda i, j: (i, j)),
                   pl.BlockSpec((tm, KV_BLOCK), lambda i, j: (i, 0))],
        out_shape=[jax.ShapeDtypeStruct((st.t, n), BF16),
                   jax.ShapeDtypeStruct((st.t, KV_BLOCK), F32)],
        scratch_shapes=[pltpu.VMEM((tm, d), BF16)],
        compiler_params=_params(("arbitrary", "arbitrary")),
        name="in_proj",
    )(x, norm_g.reshape(1, d), mods5, mods5, w_in_r)


def _gmlp_kernel(y_ref, ws_ref, bs_ref, vg_ref, o_ref):
    for c in range(y_ref.shape[0] // CHUNK):
        rows = slice(c * CHUNK, (c + 1) * CHUNK)
        for g in range(A_GROUPS):
            cols = slice(g * A_HEAD, (g + 1) * A_HEAD)
            u = _gelu_tanh(y_ref[rows, cols].astype(F32))
            v = _gelu_tanh(y_ref[rows, A_WIDTH + g * A_HEAD:A_WIDTH + (g + 1) * A_HEAD].astype(F32))
            v = _rms(v, A_HEAD) * vg_ref[:, cols]
            s = jnp.dot(ws_ref[g], v.astype(BF16), preferred_element_type=F32) + bs_ref[g]
            o_ref[rows, cols] = (u * s).astype(BF16)


def _gmlp(st, y, gm_ws, gm_bs, gm_vg):
    tm = st.tm
    bias = jnp.broadcast_to(gm_bs[:, :, None], (A_GROUPS, CHUNK, A_HEAD))
    return pl.pallas_call(
        _gmlp_kernel,
        grid=(st.n_tiles,),
        in_specs=[pl.BlockSpec((tm, 2 * A_WIDTH), lambda i: (i, COL_A // (2 * A_WIDTH))),
                  pl.BlockSpec((A_GROUPS, CHUNK, CHUNK), lambda i: (0, 0, 0)),
                  pl.BlockSpec((A_GROUPS, CHUNK, A_HEAD), lambda i: (0, 0, 0)),
                  pl.BlockSpec((1, A_WIDTH), lambda i: (0, 0))],
        out_specs=pl.BlockSpec((tm, A_WIDTH), lambda i: (i, 0)),
        out_shape=jax.ShapeDtypeStruct((st.t, A_WIDTH), BF16),
        compiler_params=_params(("arbitrary",)),
        name="gmlp",
    )(y, gm_ws.astype(BF16), bias, gm_vg.reshape(1, A_WIDTH))


def _rotate(b, c, s):
    return b * c + pltpu.roll(b, LANES // 2, 1) * s


def _mla_q_kernel(y_ref, qg_ref, w_ref, qn_ref, c_ref, s_ref, o_ref):
    x = y_ref[...].astype(F32)
    xn = _rms(x, Q_LORA) * qg_ref[...]
    q = jnp.dot(xn.astype(BF16), w_ref[...], preferred_element_type=F32)
    scale = QK_DIM ** -0.5
    c, s = c_ref[...], s_ref[...]
    for h in range(MLA_HEADS):
        a = q[:, h * HEAD_PAD:h * HEAD_PAD + LANES]
        b = q[:, h * HEAD_PAD + LANES:(h + 1) * HEAD_PAD]
        ssq = jnp.sum(a * a, axis=-1, keepdims=True) + jnp.sum(b * b, axis=-1, keepdims=True)
        inv = lax.rsqrt(ssq * (1.0 / QK_DIM) + EPS)
        a = a * inv * qn_ref[:, :LANES]
        b = _rotate(b * inv * qn_ref[:, LANES:], c, s)
        o_ref[:, h * HEAD_PAD:h * HEAD_PAD + LANES] = (a * scale).astype(BF16)
        o_ref[:, h * HEAD_PAD + LANES:(h + 1) * HEAD_PAD] = (b * scale).astype(BF16)


def _mla_q(st, y, q_g, w_qb_p, qn_p, rope_c, rope_s):
    tm = st.tm
    width = MLA_HEADS * HEAD_PAD
    return pl.pallas_call(
        _mla_q_kernel,
        grid=(st.n_tiles,),
        in_specs=[pl.BlockSpec((tm, Q_LORA), lambda i: (i, COL_Q // Q_LORA)),
                  pl.BlockSpec((1, Q_LORA), lambda i: (0, 0)),
                  pl.BlockSpec((Q_LORA, width), lambda i: (0, 0)),
                  pl.BlockSpec((1, HEAD_PAD), lambda i: (0, 0)),
                  pl.BlockSpec((tm, LANES), lambda i: (st.rope_block(i), 0)),
                  pl.BlockSpec((tm, LANES), lambda i: (st.rope_block(i), 0))],
        out_specs=pl.BlockSpec((tm, width), lambda i: (i, 0)),
        out_shape=jax.ShapeDtypeStruct((st.t, width), BF16),
        compiler_params=_params(("arbitrary",)),
        name="mla_q",
    )(y, q_g.reshape(1, Q_LORA), w_qb_p, qn_p.reshape(1, HEAD_PAD), rope_c, rope_s)


def _keys_values(x, kr, w_ref, kn_ref, k_ref, v_ref, rope):
    kv = jnp.dot(x.astype(BF16), w_ref[...], preferred_element_type=F32)
    v_ref[...] = kv[:, MLA_HEADS * NOPE_DIM:].astype(BF16)
    kr_sq = jnp.sum(kr * kr, axis=-1, keepdims=True)
    for h in range(MLA_HEADS):
        a = kv[:, h * NOPE_DIM:(h + 1) * NOPE_DIM]
        inv = lax.rsqrt((jnp.sum(a * a, axis=-1, keepdims=True) + kr_sq) * (1.0 / QK_DIM) + EPS)
        b = kr * inv * kn_ref[:, LANES:]
        if rope is not None:
            b = _rotate(b, rope[0], rope[1])
        k_ref[:, h * HEAD_PAD:h * HEAD_PAD + LANES] = (a * inv * kn_ref[:, :LANES]).astype(BF16)
        k_ref[:, h * HEAD_PAD + LANES:(h + 1) * HEAD_PAD] = b.astype(BF16)


def _mla_kv_kernel(m_kv_ref, kr_ref, kvg_ref, w_ref, kn_ref, c_ref, s_ref, ckv_ref, k_ref, v_ref):
    x = _rms(m_kv_ref[...], KV_LORA) * kvg_ref[...]
    ckv_ref[...] = x
    _keys_values(x, kr_ref[...], w_ref, kn_ref, k_ref, v_ref, (c_ref[...], s_ref[...]))


def _mla_kv(st, kvblk, kv_g, w_kvb_r, kn_p, rope_c, rope_s):
    tm = st.tm
    kw, vw = MLA_HEADS * HEAD_PAD, MLA_HEADS * V_DIM
    return pl.pallas_call(
        _mla_kv_kernel,
        grid=(st.n_tiles,),
        in_specs=[pl.BlockSpec((tm, KV_LORA), lambda i: (i, 0)),
                  pl.BlockSpec((tm, LANES), lambda i: (i, KV_BLOCK // LANES - 1)),
                  pl.BlockSpec((1, KV_LORA), lambda i: (0, 0)),
                  pl.BlockSpec((KV_LORA, w_kvb_r.shape[1]), lambda i: (0, 0)),
                  pl.BlockSpec((1, HEAD_PAD), lambda i: (0, 0)),
                  pl.BlockSpec((tm, LANES), lambda i: (st.rope_block(i), 0)),
                  pl.BlockSpec((tm, LANES), lambda i: (st.rope_block(i), 0))],
        out_specs=[pl.BlockSpec((tm, KV_LORA), lambda i: (i, 0)),
                   pl.BlockSpec((tm, kw), lambda i: (i, 0)),
                   pl.BlockSpec((tm, vw), lambda i: (i, 0))],
        out_shape=[jax.ShapeDtypeStruct((st.t, KV_LORA), F32),
                   jax.ShapeDtypeStruct((st.t, kw), BF16),
                   jax.ShapeDtypeStruct((st.t, vw), BF16)],
        compiler_params=_params(("arbitrary",)),
        name="mla_kv",
    )(kvblk, kvblk, kv_g.reshape(1, KV_LORA), w_kvb_r, kn_p.reshape(1, HEAD_PAD), rope_c, rope_s)


def _ctx_kv_kernel(ckv_ref, kr_ref, w_ref, kn_ref, k_ref, v_ref):
    _keys_values(ckv_ref[...], kr_ref[...], w_ref, kn_ref, k_ref, v_ref, None)


def _ctx_kv(ckv_c, kr_c_block, w_kvb_r, kn_p):
    rows = ckv_c.shape[0]
    tm = min(TOKEN_TILE, rows)
    assert rows % tm == 0
    kw, vw = MLA_HEADS * HEAD_PAD, MLA_HEADS * V_DIM
    return pl.pallas_call(
        _ctx_kv_kernel,
        grid=(rows // tm,),
        in_specs=[pl.BlockSpec((tm, KV_LORA), lambda i: (i, 0)),
                  pl.BlockSpec((tm, LANES), lambda i: (i, 0)),
                  pl.BlockSpec((KV_LORA, w_kvb_r.shape[1]), lambda i: (0, 0)),
                  pl.BlockSpec((1, HEAD_PAD), lambda i: (0, 0))],
        out_specs=[pl.BlockSpec((tm, kw), lambda i: (i, 0)),
                   pl.BlockSpec((tm, vw), lambda i: (i, 0))],
        out_shape=[jax.ShapeDtypeStruct((rows, kw), BF16),
                   jax.ShapeDtypeStruct((rows, vw), BF16)],
        compiler_params=_params(("arbitrary",)),
        name="ctx_kv",
    )(ckv_c, kr_c_block, w_kvb_r, kn_p.reshape(1, HEAD_PAD))


_NT_DIMS = (((1,), (1,)), ((), ()))
_TN_DIMS = (((0,), (0,)), ((), ()))


def _attn_kernel(*refs, has_ctx):
    if has_ctx:
        q_ref, k_ref, v_ref, kc_ref, vc_ref, o_ref = refs
    else:
        q_ref, k_ref, v_ref, o_ref = refs
    q = q_ref[...]
    s = lax.dot_general(q, k_ref[...], _NT_DIMS, preferred_element_type=F32)
    m = jnp.max(s, axis=-1, keepdims=True)
    if has_ctx:
        sc = lax.dot_general(q, kc_ref[...], _NT_DIMS, preferred_element_type=F32)
        m = jnp.maximum(m, jnp.max(sc, axis=-1, keepdims=True))
    p = jnp.exp(s - m)
    l = jnp.sum(p, axis=-1, keepdims=True)
    o = jnp.dot(p.astype(BF16), v_ref[...], preferred_element_type=F32)
    if has_ctx:
        pc = jnp.exp(sc - m)
        l = l + jnp.sum(pc, axis=-1, keepdims=True)
        o = o + jnp.dot(pc.astype(BF16), vc_ref[...], preferred_element_type=F32)
    o_ref[...] = (o / l).astype(BF16)


def _attention(q, k, v, first_row, batch, length, ctx=None):
    tq = min(TOKEN_TILE, length)
    assert length % tq == 0 and first_row % length == 0
    nq = length // tq
    qb0, sb0 = first_row // tq, first_row // length
    in_specs = [pl.BlockSpec((tq, HEAD_PAD), lambda b, h, i: (qb0 + b * nq + i, h)),
                pl.BlockSpec((length, HEAD_PAD), lambda b, h, i: (sb0 + b, h)),
                pl.BlockSpec((length, V_DIM), lambda b, h, i: (sb0 + b, h))]
    args = [q, k, v]
    if ctx is not None:
        kc, vc = ctx
        past = kc.shape[0] // batch
        in_specs += [pl.BlockSpec((past, HEAD_PAD), lambda b, h, i: (b, h)),
                     pl.BlockSpec((past, V_DIM), lambda b, h, i: (b, h))]
        args += [kc, vc]
    return pl.pallas_call(
        functools.partial(_attn_kernel, has_ctx=ctx is not None),
        grid=(batch, MLA_HEADS, nq),
        in_specs=in_specs,
        out_specs=pl.BlockSpec((tq, V_DIM), lambda b, h, i: (b * nq + i, h)),
        out_shape=jax.ShapeDtypeStruct((batch * length, MLA_HEADS * V_DIM), BF16),
        compiler_params=_params(("arbitrary", "arbitrary", "arbitrary")),
        name="attention_ctx" if ctx is not None else "attention",
    )(*args)


def _log_sigmoid(x):
    return jnp.minimum(x, 0.0) - jnp.log(1.0 + jnp.exp(-jnp.abs(x)))


def _retention_kernel(*refs, n_chunks, has_state):
    if has_state:
        dec_ref, q_ref, k_ref, v_ref, g_ref, rg_ref, s0_ref, o_ref, sf_scr, sb_scr = refs
    else:
        dec_ref, q_ref, k_ref, v_ref, g_ref, rg_ref, o_ref, st_ref, sf_scr, sb_scr = refs
    c = CHUNK
    lg_f = _log_sigmoid(dec_ref[0])
    lg_b = _log_sigmoid(dec_ref[1])
    row = lax.broadcasted_iota(jnp.int32, (c, c), 0).astype(F32)
    col = lax.broadcasted_iota(jnp.int32, (c, c), 1).astype(F32)
    diff = row - col
    k_scale = RET_DK ** -0.5
    mask = (jnp.where(diff >= 0, jnp.exp(lg_f * jnp.maximum(diff, 0.0)), 0.0)
            + jnp.where(diff <= 0, jnp.exp(lg_b * jnp.maximum(-diff, 0.0)), 0.0)) * k_scale
    q_dec_f = jnp.exp(lg_f * (row + 1.0))
    q_dec_b = jnp.exp(lg_b * (c - row))
    k_dec_f = jnp.exp(lg_f * (c - 1.0 - row)) * k_scale
    k_dec_b = jnp.exp(lg_b * row) * k_scale
    c_dec_f = jnp.exp(lg_f * c)
    c_dec_b = jnp.exp(lg_b * c)

    def rows(j):
        return slice(j * c, (j + 1) * c)

    def outer(j, k_dec):
        kd = (k_ref[rows(j), :].astype(F32) * k_dec).astype(BF16)
        return lax.dot_general(kd, v_ref[rows(j), :], _TN_DIMS, preferred_element_type=F32)

    s = s0_ref[0] if has_state else jnp.zeros((RET_DK, RET_DV), F32)
    for j in range(n_chunks):
        sf_scr[j] = s
        s = s * c_dec_f + outer(j, k_dec_f)
    if not has_state:
        st_ref[0] = s
    s = s0_ref[1] if has_state else jnp.zeros((RET_DK, RET_DV), F32)
    for j in reversed(range(n_chunks)):
        sb_scr[j] = s
        s = s * c_dec_b + outer(j, k_dec_b)
    if not has_state:
        st_ref[1] = s

    for j in range(n_chunks):
        qc = q_ref[rows(j), :]
        qf = qc.astype(F32)
        att = lax.dot_general(qc, k_ref[rows(j), :], _NT_DIMS, preferred_element_type=F32) * mask
        o = jnp.dot(att.astype(BF16), v_ref[rows(j), :], preferred_element_type=F32)
        o = o + jnp.dot((qf * q_dec_f).astype(BF16), sf_scr[j].astype(BF16), preferred_element_type=F32)
        o = o + jnp.dot((qf * q_dec_b).astype(BF16), sb_scr[j].astype(BF16), preferred_element_type=F32)
        y = _rms(o, RET_DV) * rg_ref[...] * _silu(g_ref[rows(j), :].astype(F32))
        o_ref[rows(j), :] = y.astype(BF16)


def _retention(y, dec_b, ret_g, first_row, batch, length, state=None, layer=0):
    assert first_row % length == 0
    sb0 = first_row // length
    n_chunks = length // CHUNK
    cb = COL_R // RET_DK
    has_state = state is not None
    in_specs = [pl.BlockSpec((2, None, 1, LANES), lambda b, h: (0, h, 0, 0)),
                pl.BlockSpec((length, RET_DK), lambda b, h: (sb0 + b, cb + h)),
                pl.BlockSpec((length, RET_DK), lambda b, h: (sb0 + b, cb + RET_HEADS + h)),
                pl.BlockSpec((length, RET_DV), lambda b, h: (sb0 + b, cb + 2 * RET_HEADS + h)),
                pl.BlockSpec((length, RET_DV), lambda b, h: (sb0 + b, cb + 3 * RET_HEADS + h)),
                pl.BlockSpec((1, RET_DV), lambda b, h: (0, h))]
    args = [dec_b, y, y, y, y, ret_g.reshape(1, RET_HEADS * RET_DV)]
    out_specs = [pl.BlockSpec((length, RET_DV), lambda b, h: (b, h))]
    out_shape = [jax.ShapeDtypeStruct((batch * length, RET_HEADS * RET_DV), BF16)]
    if has_state:
        in_specs.append(pl.BlockSpec((None, None, 2, None, RET_DK, RET_DV),
                                     lambda b, h: (b, layer, 0, h, 0, 0)))
        args.append(state)
    else:
        out_specs.append(pl.BlockSpec((None, 2, None, RET_DK, RET_DV), lambda b, h: (b, 0, h, 0, 0)))
        out_shape.append(jax.ShapeDtypeStruct((batch, 2, RET_HEADS, RET_DK, RET_DV), F32))
    return pl.pallas_call(
        functools.partial(_retention_kernel, n_chunks=n_chunks, has_state=has_state),
        grid=(batch, RET_HEADS),
        in_specs=in_specs,
        out_specs=out_specs,
        out_shape=out_shape,
        scratch_shapes=[pltpu.VMEM((n_chunks, RET_DK, RET_DV), F32),
                        pltpu.VMEM((n_chunks, RET_DK, RET_DV), F32)],
        compiler_params=_params(("arbitrary", "arbitrary")),
        name="retention_state" if has_state else "retention",
    )(*args)


def _out_proj_kernel(ya_ref, yb0_ref, yb1_ref, yc_ref, w0_ref, w1_ref, w2_ref, w3_ref, x_ref, g_ref, o_ref):
    acc = jnp.dot(ya_ref[...], w0_ref[...], preferred_element_type=F32)
    acc = acc + jnp.dot(yb0_ref[...], w1_ref[...], preferred_element_type=F32)
    acc = acc + jnp.dot(yb1_ref[...], w2_ref[...], preferred_element_type=F32)
    acc = acc + jnp.dot(yc_ref[...], w3_ref[...], preferred_element_type=F32)
    o_ref[...] = x_ref[...] + g_ref[...] * acc


def _out_proj(st, ya, yb, yc, w_o, x, mods5, layer):
    tm, d = st.tm, st.d
    tn = 1024
    kb = 512
    assert ya.shape[1] == kb and yb.shape[1] == 2 * kb and yc.shape[1] == kb and w_o.shape[0] == 4 * kb

    def w_spec(r):
        return pl.BlockSpec((kb, tn), lambda i, j: (r, j))

    return pl.pallas_call(
        _out_proj_kernel,
        grid=(st.n_tiles, d // tn),
        in_specs=[pl.BlockSpec((tm, kb), lambda i, j: (i, 0)),
                  pl.BlockSpec((tm, kb), lambda i, j: (i, 0)),
                  pl.BlockSpec((tm, kb), lambda i, j: (i, 1)),
                  pl.BlockSpec((tm, kb), lambda i, j: (i, 0)),
                  w_spec(0), w_spec(1), w_spec(2), w_spec(3),
                  pl.BlockSpec((tm, tn), lambda i, j: (i, j)),
                  pl.BlockSpec((None, None, None, 1, tn), lambda i, j: (layer, 2, st.mod_row(i), 0, j))],
        out_specs=pl.BlockSpec((tm, tn), lambda i, j: (i, j)),
        out_shape=jax.ShapeDtypeStruct((st.t, d), F32),
        compiler_params=_params(("arbitrary", "arbitrary")),
        name="out_proj",
    )(ya, yb, yb, yc, w_o, w_o, w_o, w_o, x, mods5)


def _router_kernel(x_ref, n_ref, sh_ref, sc_ref, wr_ref, br_ref, hp_ref, rw_ref, ri_ref):
    x = x_ref[...]
    h = _rms(x, x.shape[1]) * n_ref[...] * (1.0 + sc_ref[...]) + sh_ref[...]
    hp_ref[...] = _pack_bf16_pairs(h)
    logits = jnp.dot(h, wr_ref[...], precision=lax.Precision.HIGHEST, preferred_element_type=F32) + br_ref[...]
    lane = lax.broadcasted_iota(jnp.int32, logits.shape, 1)
    lane_f = lane.astype(F32)
    neg = -jnp.inf
    g_mask = lane < N_GROUPS
    g_max = jnp.max(jnp.where(g_mask, logits, neg), axis=-1, keepdims=True)
    g_sum = jnp.sum(jnp.where(g_mask, jnp.exp(logits - g_max), 0.0), axis=-1, keepdims=True)
    g_p = 1.0 / g_sum
    g_idx = jnp.min(jnp.where(g_mask & (logits == g_max), lane_f, float(LANES)), axis=-1, keepdims=True)
    e_lane = lane - N_GROUPS
    e_mask = (e_lane >= 0) & (e_lane < N_EXPERTS) & ((e_lane >> 3).astype(F32) == g_idx)
    m1 = jnp.max(jnp.where(e_mask, logits, neg), axis=-1, keepdims=True)
    i1 = jnp.min(jnp.where(e_mask & (logits == m1), lane_f, float(LANES)), axis=-1, keepdims=True)
    rest = e_mask & (lane_f != i1)
    m2 = jnp.max(jnp.where(rest, logits, neg), axis=-1, keepdims=True)
    i2 = jnp.min(jnp.where(rest & (logits == m2), lane_f, float(LANES)), axis=-1, keepdims=True)
    t2 = jnp.exp(m2 - m1)
    w1 = g_p / (1.0 + t2)
    w2 = g_p * t2 / (1.0 + t2)
    rw_ref[...] = jnp.where(lane == 0, w1, jnp.where(lane == 1, w2, 0.0))
    ids = jnp.where(lane == 0, i1, jnp.where(lane == 1, i2, float(N_GROUPS))) - float(N_GROUPS)
    ri_ref[...] = ids.astype(jnp.int32)


def _router(st, x, norm_g, mods5, layer, w_r, b_r):
    tm, d = st.tm, st.d
    return pl.pallas_call(
        _router_kernel,
        grid=(st.n_tiles,),
        in_specs=[pl.BlockSpec((tm, d), lambda i: (i, 0)),
                  pl.BlockSpec((1, d), lambda i: (0, 0)),
                  st.mod_spec(layer, 3, with_j=False), st.mod_spec(layer, 4, with_j=False),
                  pl.BlockSpec((d, LANES), lambda i: (0, 0)),
                  pl.BlockSpec((1, LANES), lambda i: (0, 0))],
        out_specs=[pl.BlockSpec((tm, d // 2), lambda i: (i, 0)),
                   pl.BlockSpec((tm, LANES), lambda i: (i, 0)),
                   pl.BlockSpec((tm, LANES), lambda i: (i, 0))],
        out_shape=[jax.ShapeDtypeStruct((st.t, d // 2), U32),
                   jax.ShapeDtypeStruct((st.t, LANES), F32),
                   jax.ShapeDtypeStruct((st.t, LANES), jnp.int32)],
        compiler_params=_params(("arbitrary",)),
        name="router",
    )(x, norm_g.reshape(1, d), mods5, mods5, w_r, b_r)


def _gather_kernel(idx_ref, src_ref, dst_ref, sem, *, chunk):
    base = pl.program_id(0) * chunk

    def copy(src_row, dst_row):
        return pltpu.make_async_copy(src_ref.at[pl.ds(src_row, 1), :], dst_ref.at[pl.ds(dst_row, 1), :], sem)

    def start(r, carry):
        copy(idx_ref[0, r], base + r).start()
        return carry

    def wait(r, carry):
        copy(0, base).wait()
        return carry

    lax.fori_loop(0, chunk, start, 0)
    lax.fori_loop(0, chunk, wait, 0)


def _gather_rows(src, idx):
    n = idx.shape[0]
    chunk = GATHER_CHUNK
    assert n % chunk == 0
    return pl.pallas_call(
        functools.partial(_gather_kernel, chunk=chunk),
        grid=(n // chunk,),
        in_specs=[pl.BlockSpec((None, 1, chunk), lambda i: (i, 0, 0), memory_space=pltpu.SMEM),
                  pl.BlockSpec(memory_space=pl.ANY)],
        out_specs=pl.BlockSpec(memory_space=pl.ANY),
        out_shape=jax.ShapeDtypeStruct((n, src.shape[1]), src.dtype),
        scratch_shapes=[pltpu.SemaphoreType.DMA],
        compiler_params=_params(("arbitrary",)),
        name="gather_rows",
    )(idx.reshape(n // chunk, 1, chunk), src)


def _expert_kernel(te_ref, tv_ref, x_ref, w1_ref, w3_ref, w2_ref, o_ref, w1_scr, w3_scr, w2_scr):
    i = pl.program_id(0)
    fresh = jnp.logical_or(i == 0, te_ref[i] != te_ref[jnp.maximum(i - 1, 0)])

    @pl.when(fresh)
    def _():
        w1_scr[...] = w1_ref[...].astype(BF16)
        w3_scr[...] = w3_ref[...].astype(BF16)
        w2_scr[...] = w2_ref[...].astype(BF16)

    @pl.when(tv_ref[i] > 0)
    def _():
        p = x_ref[...]
        half = p.shape[1]
        lo = _unpack_lo(p).astype(BF16)
        hi = _unpack_hi(p).astype(BF16)
        a = (jnp.dot(lo, w1_scr[:half, :], preferred_element_type=F32)
             + jnp.dot(hi, w1_scr[half:, :], preferred_element_type=F32))
        b = (jnp.dot(lo, w3_scr[:half, :], preferred_element_type=F32)
             + jnp.dot(hi, w3_scr[half:, :], preferred_element_type=F32))
        hid = (_silu(a) * b).astype(BF16)
        o_ref[...] = _pack_bf16_pairs(jnp.dot(hid, w2_scr[...], preferred_element_type=F32))

    @pl.when(tv_ref[i] == 0)
    def _():
        o_ref[...] = jnp.zeros(o_ref.shape, o_ref.dtype)


def _experts(xs, tile_expert, tile_valid, w1, w3, w2, layer):
    rows, half = xs.shape
    tm = EXPERT_TILE
    d, de = w1.shape[2], w1.shape[3]
    grid_spec = pltpu.PrefetchScalarGridSpec(
        num_scalar_prefetch=2,
        grid=(rows // tm,),
        in_specs=[pl.BlockSpec((tm, half), lambda i, te, tv: (i, 0)),
                  pl.BlockSpec((None, None, d, de), lambda i, te, tv: (layer, te[i], 0, 0)),
                  pl.BlockSpec((None, None, d, de), lambda i, te, tv: (layer, te[i], 0, 0)),
                  pl.BlockSpec((None, None, de, d), lambda i, te, tv: (layer, te[i], 0, 0))],
        out_specs=pl.BlockSpec((tm, half), lambda i, te, tv: (i, 0)),
        scratch_shapes=[pltpu.VMEM((d, de), BF16), pltpu.VMEM((d, de), BF16), pltpu.VMEM((de, d), BF16)],
    )
    return pl.pallas_call(
        _expert_kernel,
        grid_spec=grid_spec,
        out_shape=jax.ShapeDtypeStruct((rows, half), U32),
        compiler_params=_params(("arbitrary",), vmem_mib=52),
        name="experts",
    )(tile_expert, tile_valid, xs, w1, w3, w2)


def _combine_kernel(x_ref, ya_ref, yb_ref, rw_ref, g_ref, o_ref):
    half = ya_ref.shape[1]
    w1 = rw_ref[:, 0:1]
    w2 = rw_ref[:, 1:2]
    pa, pb = ya_ref[...], yb_ref[...]
    o_ref[:, :half] = x_ref[:, :half] + g_ref[:, :half] * (w1 * _unpack_lo(pa) + w2 * _unpack_lo(pb))
    o_ref[:, half:] = x_ref[:, half:] + g_ref[:, half:] * (w1 * _unpack_hi(pa) + w2 * _unpack_hi(pb))


def _combine(st, x, yg, rw, mods5, layer):
    tm, d = st.tm, st.d
    nt = st.n_tiles
    return pl.pallas_call(
        _combine_kernel,
        grid=(nt,),
        in_specs=[pl.BlockSpec((tm, d), lambda i: (i, 0)),
                  pl.BlockSpec((tm, d // 2), lambda i: (i, 0)),
                  pl.BlockSpec((tm, d // 2), lambda i: (nt + i, 0)),
                  pl.BlockSpec((tm, LANES), lambda i: (i, 0)),
                  st.mod_spec(layer, 5, with_j=False)],
        out_specs=pl.BlockSpec((tm, d), lambda i: (i, 0)),
        out_shape=jax.ShapeDtypeStruct((st.t, d), F32),
        compiler_params=_params(("arbitrary",)),
        name="combine",
    )(x, yg, yg, rw, mods5)


def _dispatch_plan(expert_ids, n_tokens):
    tm = EXPERT_TILE
    n_assign = 2 * n_tokens
    n_tiles = n_assign // tm + N_EXPERTS
    flat = expert_ids.reshape(-1)
    onehot = (flat[:, None] == jnp.arange(N_EXPERTS, dtype=jnp.int32)[None, :]).astype(jnp.int32)
    counts = jnp.sum(onehot, axis=0)
    rank = jnp.sum((jnp.cumsum(onehot, axis=0) - onehot) * onehot, axis=1)
    padded = ((counts + tm - 1) // tm) * tm
    pad_end = jnp.cumsum(padded)
    pad_start = pad_end - padded
    start = jnp.cumsum(counts) - counts
    pos = pad_start[flat] + rank
    order = jnp.argsort(flat, stable=True).astype(jnp.int32)
    tile_first = jnp.arange(n_tiles, dtype=jnp.int32) * tm
    tile_expert = jnp.minimum(jnp.searchsorted(pad_end, tile_first, side="right"), N_EXPERTS - 1).astype(jnp.int32)
    tile_valid = (tile_first < pad_end[-1]).astype(jnp.int32)
    row = jnp.arange(n_tiles * tm, dtype=jnp.int32)
    row_e = jnp.repeat(tile_expert, tm)
    within = row - pad_start[row_e]
    live = (within < counts[row_e]) & (jnp.repeat(tile_valid, tm) > 0)
    src_token = jnp.where(live, order[jnp.clip(start[row_e] + within, 0, n_assign - 1)] // 2, 0).astype(jnp.int32)
    back = pos.reshape(n_tokens, 2).T.reshape(-1).astype(jnp.int32)
    return src_token, back, tile_expert, tile_valid


def kernel(x_prompt, x_sample, cache_mla_ckv, cache_mla_krope, state_ret, c, c_ctx, norm1_g, norm2_g, mod_w, mod_b, w_in, gm_v_g, gm_ws, gm_bs, mla_q_g, mla_w_qb, mla_kv_g, mla_w_kvb, mla_qn_g, mla_kn_g, ret_decay, ret_g, w_out, moe_wg, moe_bg, moe_we, moe_be, moe_w1, moe_w3, moe_w2):
    batch, seq, d = x_prompt.shape
    dec_batch, dec_seq, _ = x_sample.shape
    depth = w_in.shape[0]
    past = cache_mla_ckv.shape[2]
    st = _Stream(batch, seq, dec_batch, dec_seq, d)

    n_cond = 1 + dec_batch
    cond_rows = -(-n_cond // 8) * 8
    cond = jnp.concatenate([c_ctx[None, :], c, jnp.zeros((cond_rows - n_cond, d), F32)], axis=0)
    mods = _modulations(cond, mod_w, mod_b)
    mods5 = mods.reshape(depth, cond_rows, N_MOD, 1, d).transpose(0, 2, 1, 3, 4)

    rope_src = _rope_lane_sources()
    head_src = _head_pad_sources()
    rope_c, rope_s = _rope_tables(st.tm, dec_seq)
    nope_cols = np.concatenate([h * (NOPE_DIM + V_DIM) + np.arange(NOPE_DIM) for h in range(MLA_HEADS)])
    v_cols = np.concatenate([h * (NOPE_DIM + V_DIM) + NOPE_DIM + np.arange(V_DIM) for h in range(MLA_HEADS)])
    gain_src = np.concatenate([np.arange(NOPE_DIM), np.where(rope_src < 0, -1, NOPE_DIM + rope_src)]).astype(np.int32)
    splits = np.cumsum([A_WIDTH, A_WIDTH, Q_LORA, KV_LORA, ROPE_DIM, 512, 512, 512, 512])[:-1]

    x = jnp.concatenate([x_prompt.reshape(st.n_prompt, d), x_sample.reshape(st.n_sample, d)], axis=0)
    ckv_out, kr_out, state_out = [], [], []
    for l in range(depth):
        a_u, a_v, m_q, m_kv, m_kr, r_q, r_k, r_v, r_gate = jnp.split(w_in[l], splits, axis=1)
        w_in_r = jnp.concatenate(
            [a_u, a_v, m_q, m_kv, m_kr, jnp.zeros((d, LANES - ROPE_DIM), F32), _take_cols(m_kr, rope_src),
             r_q, r_k, r_v, r_gate], axis=1).astype(BF16)
        w_qb_p = _take_cols(mla_w_qb[l], head_src).astype(BF16)
        w_kvb_r = jnp.concatenate([mla_w_kvb[l][:, nope_cols], mla_w_kvb[l][:, v_cols]], axis=1).astype(BF16)
        qn_p = _take_cols(mla_qn_g[l], gain_src)
        kn_p = _take_cols(mla_kn_g[l], gain_src)
        dec_b = jnp.broadcast_to(ret_decay[l][:, :, None, None], (2, RET_HEADS, 1, LANES))
        w_r = jnp.concatenate([moe_wg[l], moe_we[l], jnp.zeros((d, LANES - N_GROUPS - N_EXPERTS), F32)], axis=1)
        b_r = jnp.concatenate([moe_bg[l], moe_be[l], jnp.zeros((LANES - N_GROUPS - N_EXPERTS,), F32)]).reshape(1, LANES)

        y, kvblk = _in_proj(st, x, norm1_g[l], mods5, l, w_in_r)
        ya = _gmlp(st, y, gm_ws[l], gm_bs[l], gm_v_g[l])
        q = _mla_q(st, y, mla_q_g[l], w_qb_p, qn_p, rope_c, rope_s)
        ckv, k, v = _mla_kv(st, kvblk, mla_kv_g[l], w_kvb_r, kn_p, rope_c, rope_s)
        kr_c = _take_cols(cache_mla_krope[:, l].reshape(dec_batch * past, ROPE_DIM), rope_src)
        ctx = _ctx_kv(cache_mla_ckv[:, l].reshape(dec_batch * past, KV_LORA), kr_c, w_kvb_r, kn_p)
        yb = jnp.concatenate([_attention(q, k, v, 0, batch, seq),
                              _attention(q, k, v, st.n_prompt, dec_batch, dec_seq, ctx)], axis=0)
        yc_p, states = _retention(y, dec_b, ret_g[l], 0, batch, seq)
        yc_s, = _retention(y, dec_b, ret_g[l], st.n_prompt, dec_batch, dec_seq, state_ret, l)
        yc = jnp.concatenate([yc_p, yc_s], axis=0)
        x = _out_proj(st, ya, yb, yc, w_out[l].astype(BF16), x, mods5, l)
        ckv_out.append(ckv[:st.n_prompt].reshape(batch, seq, KV_LORA))
        kr_out.append(kvblk[:st.n_prompt, KV_LORA:KV_LORA + ROPE_DIM].reshape(batch, seq, ROPE_DIM))
        state_out.append(states)

        hp, rw, ri = _router(st, x, norm2_g[l], mods5, l, w_r, b_r)
        src_token, back, tile_expert, tile_valid = _dispatch_plan(ri[:, :2], st.t)
        xs = _gather_rows(hp, src_token)
        ys = _experts(xs, tile_expert, tile_valid, moe_w1, moe_w3, moe_w2, l)
        yg = _gather_rows(ys, back)
        x = _combine(st, x, yg, rw, mods5, l)

    return (x[:st.n_prompt].reshape(batch, seq, d), x[st.n_prompt:].reshape(dec_batch, dec_seq, d),
            jnp.stack(ckv_out, axis=1), jnp.stack(kr_out, axis=1), jnp.stack(state_out, axis=1))
```
